```python
import math
import jax, jax.numpy as jnp
from jax import lax
import numpy as np

D_MODEL = 1024
BATCH = 4
SEQ = 8192
DEPTH = 1

D_MIX = D_MODEL
HEAD_DIM = 64
N_Q_HEADS = 8
N_KV_HEADS = 2
Q_PER_KV = N_Q_HEADS // N_KV_HEADS
D_ATTN = N_Q_HEADS * HEAD_DIM
D_CONV = D_MIX - D_ATTN
CONV_WIDTH = 31
WINDOW = 128
BLOCK = 128
N_BUCKETS = 32
MAX_EXACT = N_BUCKETS // 2
MAX_DISTANCE = 128
D_FF = int(math.ceil(8 * D_MODEL / 3 / 256) * 256)
EPS = 1e-6
D_Q = N_Q_HEADS * HEAD_DIM
D_KV = N_KV_HEADS * HEAD_DIM
D_IN = D_Q + 2 * D_KV + 2 * D_CONV

kernel_name = "hymba_conformer_swa_sink_hybrid"


def rms_norm(x, g):
    xf = x.astype(jnp.float32)
    y = xf * lax.rsqrt(jnp.mean(xf * xf, axis=-1, keepdims=True) + EPS)
    return (y * g.astype(jnp.float32)).astype(x.dtype)


def layer_norm(x, g, b):
    xf = x.astype(jnp.float32)
    mu = jnp.mean(xf, axis=-1, keepdims=True)
    var = jnp.mean(jnp.square(xf - mu), axis=-1, keepdims=True)
    y = (xf - mu) * lax.rsqrt(var + EPS)
    return (y * g.astype(jnp.float32) + b.astype(jnp.float32)).astype(x.dtype)


def t5_causal_bucket(dist):
    is_small = dist < MAX_EXACT
    d = jnp.maximum(dist, 1).astype(jnp.float32)
    large = MAX_EXACT + (jnp.log(d / MAX_EXACT) / math.log(MAX_DISTANCE / MAX_EXACT)
                         * (N_BUCKETS - MAX_EXACT)).astype(jnp.int32)
    large = jnp.minimum(large, N_BUCKETS - 1)
    return jnp.where(is_small, dist, large)


def conformer_conv(u, conv_dw, conv_dw_b, conv_ln_g, conv_ln_b, w_conv_pw):
    a, gate = jnp.split(u, 2, axis=-1)
    h = a * jax.nn.sigmoid(gate)
    h = lax.conv_general_dilated(
        h, conv_dw[:, None, :].astype(h.dtype), window_strides=(1,),
        padding=[(CONV_WIDTH - 1, 0)],
        dimension_numbers=("NWC", "WIO", "NWC"),
        feature_group_count=D_CONV) + conv_dw_b
    h = jax.nn.silu(layer_norm(h, conv_ln_g, conv_ln_b))
    return h @ w_conv_pw


def sliding_window_attention(q, k, v, attn_sinks, rel_bias):
    B, S = q.shape[0], q.shape[1]
    nb = S // BLOCK
    qb = q.reshape(B, nb, BLOCK, N_KV_HEADS, Q_PER_KV, HEAD_DIM)
    pad = ((0, 0), (BLOCK, 0), (0, 0), (0, 0))
    kp, vp = jnp.pad(k, pad), jnp.pad(v, pad)

    def band(t):
        prev = t[:, :S].reshape(B, nb, BLOCK, N_KV_HEADS, HEAD_DIM)
        cur = t[:, BLOCK:].reshape(B, nb, BLOCK, N_KV_HEADS, HEAD_DIM)
        return jnp.concatenate([prev, cur], axis=2)

    kb, vb = band(kp), band(vp)
    scale = 1.0 / math.sqrt(HEAD_DIM)
    s = jnp.einsum("bnqhgd,bnkhd->bnhgqk", qb, kb).astype(jnp.float32) * scale

    qi = jnp.arange(BLOCK)[:, None]
    kj = jnp.arange(2 * BLOCK)[None, :]
    dist = qi + BLOCK - kj
    in_band = (dist >= 0) & (dist < WINDOW)
    bucket = t5_causal_bucket(jnp.maximum(dist, 0))
    bias = rel_bias.astype(jnp.float32)[bucket]
    bias = jnp.transpose(bias, (2, 0, 1)).reshape(N_KV_HEADS, Q_PER_KV, BLOCK, 2 * BLOCK)
    blk = jnp.arange(nb)[:, None, None]
    valid = in_band[None] & ((blk > 0) | (kj[None] >= BLOCK))
    s = jnp.where(valid[None, :, None, None], s + bias, -jnp.inf)

    sink = attn_sinks.astype(jnp.float32).reshape(N_KV_HEADS, Q_PER_KV, 1, 1)
    m = jnp.maximum(jnp.max(s, axis=-1, keepdims=True), sink)
    p = jnp.exp(s - m)
    denom = jnp.sum(p, axis=-1, keepdims=True) + jnp.exp(sink - m)
    p = (p / denom).astype(v.dtype)
    o = jnp.einsum("bnhgqk,bnkhd->bnqhgd", p, vb)
    return o.reshape(B, S, N_Q_HEADS * HEAD_DIM)


def setup_inputs(seed: int = 0) -> dict:
    key = jax.random.key(seed)
    ks = jax.random.split(key, 20)
    f = jnp.float32
    nrm = lambda k, shp, sc: jax.random.normal(k, shp, f) * sc
    gain = lambda k, n: 1.0 + 0.05 * jax.random.normal(k, (n,), f)
    return {
        "x": jax.random.normal(ks[0], (BATCH, SEQ, D_MODEL), f),
        "mix_pre_g": gain(ks[1], D_MODEL),
        "mix_post_g": gain(ks[2], D_MODEL),
        "ffn_pre_g": gain(ks[3], D_MODEL),
        "ffn_post_g": gain(ks[4], D_MODEL),
        "w_in": nrm(ks[5], (D_MODEL, D_IN), D_MODEL ** -0.5),
        "conv_dw": nrm(ks[6], (CONV_WIDTH, D_CONV), CONV_WIDTH ** -0.5),
        "conv_dw_b": nrm(ks[7], (D_CONV,), 0.02),
        "conv_ln_g": gain(ks[8], D_CONV),
        "conv_ln_b": nrm(ks[9], (D_CONV,), 0.02),
        "w_conv_pw": nrm(ks[10], (D_CONV, D_CONV), D_CONV ** -0.5),
        "attn_sinks": nrm(ks[11], (N_Q_HEADS,), 0.5),
        "rel_bias": nrm(ks[12], (N_BUCKETS, N_Q_HEADS), 0.5),
        "w_out": nrm(ks[13], (D_MIX, D_MODEL), D_MIX ** -0.5),
        "w_gate": nrm(ks[14], (D_MODEL, D_FF), D_MODEL ** -0.5),
        "w_up": nrm(ks[15], (D_MODEL, D_FF), D_MODEL ** -0.5),
        "w_down": nrm(ks[16], (D_FF, D_MODEL), D_FF ** -0.5),
    }


def reference(x, mix_pre_g, mix_post_g, ffn_pre_g, ffn_post_g, w_in, conv_dw, conv_dw_b,
              conv_ln_g, conv_ln_b, w_conv_pw, attn_sinks, rel_bias, w_out,
              w_gate, w_up, w_down):
    B, S, _ = x.shape
    h = x
    for _layer in range(DEPTH):
        hn = rms_norm(h, mix_pre_g)
        proj = hn @ w_in
        q = proj[..., :D_Q].reshape(B, S, N_Q_HEADS, HEAD_DIM)
        k = proj[..., D_Q:D_Q + D_KV].reshape(B, S, N_KV_HEADS, HEAD_DIM)
        v = proj[..., D_Q + D_KV:D_Q + 2 * D_KV].reshape(B, S, N_KV_HEADS, HEAD_DIM)
        u = proj[..., D_Q + 2 * D_KV:]
        attn_out = sliding_window_attention(q, k, v, attn_sinks, rel_bias)
        conv_out = conformer_conv(u, conv_dw, conv_dw_b, conv_ln_g, conv_ln_b, w_conv_pw)
        mixed = jnp.concatenate([attn_out, conv_out], axis=-1) @ w_out
        h = h + rms_norm(mixed, mix_post_g)
        hn = rms_norm(h, ffn_pre_g)
        ff = (jax.nn.silu(hn @ w_gate) * (hn @ w_up)) @ w_down
        h = h + rms_norm(ff, ffn_post_g)
    return h
```

```python
import math

import numpy as np
import jax
import jax.numpy as jnp
from jax import lax
from jax.experimental import pallas as pl
from jax.experimental.pallas import tpu as pltpu

F32 = jnp.float32
BF16 = jnp.bfloat16

D_MODEL = 1024
HEAD_DIM = 64
N_Q_HEADS = 8
N_KV_HEADS = 2
Q_PER_KV = N_Q_HEADS // N_KV_HEADS
D_Q = N_Q_HEADS * HEAD_DIM
D_KV = N_KV_HEADS * HEAD_DIM
D_CONV = 512
CONV_WIDTH = 31
WINDOW = 128
BLOCK = 128
N_BUCKETS = 32
MAX_EXACT = N_BUCKETS // 2
MAX_DISTANCE = 128
D_FF = 2816
EPS = 1e-6
D_IN = D_Q + 2 * D_KV + 2 * D_CONV
Q_SCALE = 1.0 / math.sqrt(HEAD_DIM)

V7X_VMEM_BYTES = 64 * 1024 * 1024
SUBLANES = 8

TM_IN = 1024
TS_MIX = 512
TM_FFN = 512
CONV_ROWS = 64
CONV_HALO = 32
FF_CHUNK = 256
VMEM_LIMIT = V7X_VMEM_BYTES - 8 * 1024 * 1024


def _bucket_table():
    qi = np.arange(BLOCK)[:, None]
    kj = np.arange(2 * BLOCK)[None, :]
    dist = qi + BLOCK - kj
    in_band = (dist >= 0) & (dist < WINDOW)
    d = np.maximum(dist, 0)
    df = np.maximum(d, 1).astype(np.float32)
    large = MAX_EXACT + (np.log(df / np.float32(MAX_EXACT))
                         / np.float32(math.log(MAX_DISTANCE / MAX_EXACT))
                         * np.float32(N_BUCKETS - MAX_EXACT)).astype(np.int32)
    large = np.minimum(large, N_BUCKETS - 1)
    bucket = np.where(d < MAX_EXACT, d, large)
    return np.where(in_band, bucket, -1).astype(np.int32)


def _bias_kernel(rb_ref, bucket_ref, out_ref):
    bucket = bucket_ref[...]
    col = lax.broadcasted_iota(jnp.int32, (BLOCK, 2 * BLOCK), 1)
    for h in range(N_Q_HEADS):
        acc = jnp.full((BLOCK, 2 * BLOCK), -jnp.inf, F32)
        for b in range(N_BUCKETS):
            acc = jnp.where(bucket == b, rb_ref[b, h], acc)
        out_ref[0, h] = acc
        out_ref[1, h] = jnp.where(col >= BLOCK, acc, -jnp.inf)


def _build_bias(rel_bias):
    return pl.pallas_call(
        _bias_kernel,
        out_shape=jax.ShapeDtypeStruct((2, N_Q_HEADS, BLOCK, 2 * BLOCK), F32),
        in_specs=[pl.BlockSpec(memory_space=pltpu.SMEM),
                  pl.BlockSpec(memory_space=pltpu.VMEM)],
        out_specs=pl.BlockSpec(memory_space=pltpu.VMEM),
        name="bias_table",
    )(rel_bias.astype(F32), jnp.asarray(_bucket_table()))


def _rms_norm(x, g):
    ms = jnp.mean(x * x, axis=-1, keepdims=True)
    return x * lax.rsqrt(ms + EPS) * g


def _inproj_kernel(x_ref, g_ref, w_ref, q_ref, kv_ref, hg_ref):
    hn = _rms_norm(x_ref[...], g_ref[...]).astype(BF16)
    q = jnp.dot(hn, w_ref[:, 0:D_Q], preferred_element_type=F32)
    q_ref[...] = (q * Q_SCALE).astype(BF16)
    kv = jnp.dot(hn, w_ref[:, D_Q:D_Q + 2 * D_KV], preferred_element_type=F32)
    kv_ref[...] = kv.astype(BF16)
    u0 = D_Q + 2 * D_KV
    a = jnp.dot(hn, w_ref[:, u0:u0 + D_CONV], preferred_element_type=F32)
    gate = jnp.dot(hn, w_ref[:, u0 + D_CONV:u0 + 2 * D_CONV], preferred_element_type=F32)
    hg_ref[...] = a * jax.nn.sigmoid(gate)


def _in_proj(x2, g, w_in):
    t = x2.shape[0]
    const = lambda i: (0, 0)
    return pl.pallas_call(
        _inproj_kernel,
        grid=(t // TM_IN,),
        in_specs=[pl.BlockSpec((TM_IN, D_MODEL), lambda i: (i, 0)),
                  pl.BlockSpec((1, D_MODEL), const),
                  pl.BlockSpec((D_MODEL, D_IN), const)],
        out_specs=[pl.BlockSpec((TM_IN, D_Q), lambda i: (i, 0)),
                   pl.BlockSpec((TM_IN, 2 * D_KV), lambda i: (i, 0)),
                   pl.BlockSpec((TM_IN, D_CONV), lambda i: (i, 0))],
        out_shape=[jax.ShapeDtypeStruct((t, D_Q), BF16),
                   jax.ShapeDtypeStruct((t, 2 * D_KV), BF16),
                   jax.ShapeDtypeStruct((t, D_CONV), F32)],
        compiler_params=pltpu.CompilerParams(
            dimension_semantics=("arbitrary",), vmem_limit_bytes=VMEM_LIMIT),
        name="in_proj",
    )(x2, g, w_in)


def _make_mixer_kernel(tiles_per_seq):
    n_blk = TS_MIX // BLOCK
    n_chunk = TS_MIX // CONV_ROWS

    def kernel(sink_ref, q_ref, kvc_ref, kvp_ref, hgc_ref, hgp_ref, x_ref, bias_ref,
               dw_ref, dwb_ref, lng_ref, lnb_ref, wpw_ref, wout_ref, g_ref, o_ref,
               hg_s, act_s, attn_s):
        first = (pl.program_id(0) % tiles_per_seq) == 0

        for j in range(n_blk):
            rows = slice(j * BLOCK, (j + 1) * BLOCK)
            q_blk = q_ref[rows, :]
            if j == 0:
                band = jnp.concatenate([kvp_ref[...], kvc_ref[0:BLOCK, :]], axis=0)
                sel = first.astype(jnp.int32)
            else:
                band = kvc_ref[(j - 1) * BLOCK:(j + 1) * BLOCK, :]
                sel = 0
            for g in range(N_KV_HEADS):
                k = band[:, g * HEAD_DIM:(g + 1) * HEAD_DIM]
                v = band[:, D_KV + g * HEAD_DIM:D_KV + (g + 1) * HEAD_DIM]
                h0 = g * Q_PER_KV
                q4 = jnp.concatenate(
                    [q_blk[:, (h0 + h) * HEAD_DIM:(h0 + h + 1) * HEAD_DIM]
                     for h in range(Q_PER_KV)], axis=0)
                s = lax.dot_general(q4, k, (((1,), (1,)), ((), ())),
                                    preferred_element_type=F32)
                bias = bias_ref[sel, h0:h0 + Q_PER_KV].reshape(Q_PER_KV * BLOCK, 2 * BLOCK)
                s = s + bias
                sink = jnp.concatenate(
                    [jnp.full((BLOCK, 1), sink_ref[h0 + h], F32) for h in range(Q_PER_KV)],
                    axis=0)
                m = jnp.maximum(jnp.max(s, axis=-1, keepdims=True), sink)
                p = jnp.exp(s - m)
                denom = jnp.sum(p, axis=-1, keepdims=True) + jnp.exp(sink - m)
                o = jnp.dot(p.astype(BF16), v, preferred_element_type=F32)
                o = o / denom
                for h in range(Q_PER_KV):
                    attn_s[rows, (h0 + h) * HEAD_DIM:(h0 + h + 1) * HEAD_DIM] = (
                        o[h * BLOCK:(h + 1) * BLOCK].astype(BF16))

        hg_s[0:CONV_HALO, :] = jnp.where(first, 0.0, hgp_ref[...])
        hg_s[CONV_HALO:, :] = hgc_ref[...]
        off = CONV_HALO - (CONV_WIDTH - 1)
        for c in range(n_chunk):
            r0 = c * CONV_ROWS
            acc = jnp.broadcast_to(dwb_ref[...], (CONV_ROWS, D_CONV))
            for w in range(CONV_WIDTH):
                acc = acc + dw_ref[w:w + 1, :] * hg_s[r0 + off + w:r0 + off + w + CONV_ROWS, :]
            mu = jnp.mean(acc, axis=-1, keepdims=True)
            cen = acc - mu
            var = jnp.mean(cen * cen, axis=-1, keepdims=True)
            y = cen * lax.rsqrt(var + EPS) * lng_ref[...] + lnb_ref[...]
            act_s[r0:r0 + CONV_ROWS, :] = (y * jax.nn.sigmoid(y)).astype(BF16)

        conv_out = jnp.dot(act_s[...], wpw_ref[...], preferred_element_type=F32)
        mixed = (jnp.dot(attn_s[...], wout_ref[0:D_Q, :], preferred_element_type=F32)
                 + jnp.dot(conv_out.astype(BF16), wout_ref[D_Q:, :], preferred_element_type=F32))
        o_ref[...] = x_ref[...] + _rms_norm(mixed, g_ref[...])

    return kernel


def _mixer(sinks, q, kv, hg, x2, bias, dw, dwb, lng, lnb, wpw, wout, g, seq_len):
    t = x2.shape[0]
    tiles_per_seq = seq_len // TS_MIX
    const2 = lambda i: (0, 0)
    kv_prev = lambda i: (jnp.maximum(i * (TS_MIX // BLOCK) - 1, 0), 0)
    hg_prev = lambda i: (jnp.maximum(i * (TS_MIX // CONV_HALO) - 1, 0), 0)
    return pl.pallas_call(
        _make_mixer_kernel(tiles_per_seq),
        grid=(t // TS_MIX,),
        in_specs=[pl.BlockSpec(memory_space=pltpu.SMEM),
                  pl.BlockSpec((TS_MIX, D_Q), lambda i: (i, 0)),
                  pl.BlockSpec((TS_MIX, 2 * D_KV), lambda i: (i, 0)),
                  pl.BlockSpec((BLOCK, 2 * D_KV), kv_prev),
                  pl.BlockSpec((TS_MIX, D_CONV), lambda i: (i, 0)),
                  pl.BlockSpec((CONV_HALO, D_CONV), hg_prev),
                  pl.BlockSpec((TS_MIX, D_MODEL), lambda i: (i, 0)),
                  pl.BlockSpec((2, N_Q_HEADS, BLOCK, 2 * BLOCK), lambda i: (0, 0, 0, 0)),
                  pl.BlockSpec((CONV_WIDTH, D_CONV), const2),
                  pl.BlockSpec((1, D_CONV), const2),
                  pl.BlockSpec((1, D_CONV), const2),
                  pl.BlockSpec((1, D_CONV), const2),
                  pl.BlockSpec((D_CONV, D_CONV), const2),
                  pl.BlockSpec((D_MODEL, D_MODEL), const2),
                  pl.BlockSpec((1, D_MODEL), const2)],
        out_specs=pl.BlockSpec((TS_MIX, D_MODEL), lambda i: (i, 0)),
        out_shape=jax.ShapeDtypeStruct((t, D_MODEL), F32),
        scratch_shapes=[pltpu.VMEM((TS_MIX + CONV_HALO, D_CONV), F32),
                        pltpu.VMEM((TS_MIX, D_CONV), BF16),
                        pltpu.VMEM((TS_MIX, D_Q), BF16)],
        compiler_params=pltpu.CompilerParams(
            dimension_semantics=("arbitrary",), vmem_limit_bytes=VMEM_LIMIT),
        name="mixer",
    )(sinks, q, kv, kv, hg, hg, x2, bias, dw, dwb, lng, lnb, wpw, wout, g)


def _ffn_kernel(h_ref, gpre_ref, gpost_ref, wg_ref, wu_ref, wd_ref, o_ref, a_s):
    h = h_ref[...]
    hn = _rms_norm(h, gpre_ref[...]).astype(BF16)
    for c in range(D_FF // FF_CHUNK):
        cols = slice(c * FF_CHUNK, (c + 1) * FF_CHUNK)
        gt = jnp.dot(hn, wg_ref[:, cols], preferred_element_type=F32)
        up = jnp.dot(hn, wu_ref[:, cols], preferred_element_type=F32)
        a_s[:, cols] = (gt * jax.nn.sigmoid(gt) * up).astype(BF16)
    ff = jnp.dot(a_s[...], wd_ref[...], preferred_element_type=F32)
    o_ref[...] = h + _rms_norm(ff, gpost_ref[...])


def _ffn(h1, gpre, gpost, wg, wu, wd):
    t = h1.shape[0]
    const = lambda i: (0, 0)
    once = pl.Buffered(1)
    return pl.pallas_call(
        _ffn_kernel,
        grid=(t // TM_FFN,),
        in_specs=[pl.BlockSpec((TM_FFN, D_MODEL), lambda i: (i, 0)),
                  pl.BlockSpec((1, D_MODEL), const),
                  pl.BlockSpec((1, D_MODEL), const),
                  pl.BlockSpec((D_MODEL, D_FF), const, pipeline_mode=once),
                  pl.BlockSpec((D_MODEL, D_FF), const, pipeline_mode=once),
                  pl.BlockSpec((D_FF, D_MODEL), const, pipeline_mode=once)],
        out_specs=pl.BlockSpec((TM_FFN, D_MODEL), lambda i: (i, 0)),
        out_shape=jax.ShapeDtypeStruct((t, D_MODEL), F32),
        scratch_shapes=[pltpu.VMEM((TM_FFN, D_FF), BF16)],
        compiler_params=pltpu.CompilerParams(
            dimension_semantics=("arbitrary",), vmem_limit_bytes=VMEM_LIMIT),
        name="ffn",
    )(h1, gpre, gpost, wg, wu, wd)


def kernel(x, mix_pre_g, mix_post_g, ffn_pre_g, ffn_post_g, w_in, conv_dw, conv_dw_b,
           conv_ln_g, conv_ln_b, w_conv_pw, attn_sinks, rel_bias, w_out,
           w_gate, w_up, w_down):
    b, s, d = x.shape
    assert d == D_MODEL and s % TS_MIX == 0 and (b * s) % TM_IN == 0 and (b * s) % TM_FFN == 0
    x2 = x.reshape(b * s, d)
    row = lambda v: v.reshape(1, -1).astype(F32)

    bias = _build_bias(rel_bias)
    q, kv, hg = _in_proj(x2, row(mix_pre_g), w_in.astype(BF16))
    h1 = _mixer(attn_sinks.astype(F32), q, kv, hg, x2, bias,
                conv_dw.astype(F32), row(conv_dw_b), row(conv_ln_g), row(conv_ln_b),
                w_conv_pw.astype(BF16), w_out.astype(BF16), row(mix_post_g), s)
    out = _ffn(h1, row(ffn_pre_g), row(ffn_post_g),
               w_gate.astype(BF16), w_up.astype(BF16), w_down.astype(BF16))
    return out.reshape(b, s, d)
```

```python
import math

import numpy as np
import jax
import jax.numpy as jnp
from jax import lax
from jax.experimental import pallas as pl
from jax.experimental.pallas import tpu as pltpu

F32 = jnp.float32
BF16 = jnp.bfloat16

D_MODEL = 1024
HEAD_DIM = 64
N_Q_HEADS = 8
N_KV_HEADS = 2
Q_PER_KV = N_Q_HEADS // N_KV_HEADS
D_Q = N_Q_HEADS * HEAD_DIM
D_KV = N_KV_HEADS * HEAD_DIM
D_CONV = 512
CONV_WIDTH = 31
WINDOW = 128
BLOCK = 128
N_BUCKETS = 32
MAX_EXACT = N_BUCKETS // 2
MAX_DISTANCE = 128
D_FF = 2816
EPS = 1e-6
D_IN = D_Q + 2 * D_KV + 2 * D_CONV
Q_SCALE = 1.0 / math.sqrt(HEAD_DIM)

V7X_VMEM_BYTES = 64 * 1024 * 1024
SUBLANES = 8
LANES = 128

TM_IN = 1024
TS_MIX = 512
TM_FFN = 512
CONV_ROWS = 128
CONV_HALO = 32
FF_CHUNK = 256
VMEM_LIMIT = V7X_VMEM_BYTES - 8 * 1024 * 1024


def _bucket_table_t():
    qi = np.arange(BLOCK)[None, :]
    kj = np.arange(2 * BLOCK)[:, None]
    dist = qi + BLOCK - kj
    in_band = (dist >= 0) & (dist < WINDOW)
    d = np.maximum(dist, 0)
    df = np.maximum(d, 1).astype(np.float32)
    large = MAX_EXACT + (np.log(df / np.float32(MAX_EXACT))
                         / np.float32(math.log(MAX_DISTANCE / MAX_EXACT))
                         * np.float32(N_BUCKETS - MAX_EXACT)).astype(np.int32)
    large = np.minimum(large, N_BUCKETS - 1)
    bucket = np.where(d < MAX_EXACT, d, large)
    return np.where(in_band, bucket, -1).astype(np.int32)


def _bias_kernel(rb_ref, bucket_ref, out_ref):
    bucket = bucket_ref[...]
    row = lax.broadcasted_iota(jnp.int32, (2 * BLOCK, BLOCK), 0)
    for h in range(N_Q_HEADS):
        acc = jnp.full((2 * BLOCK, BLOCK), -jnp.inf, F32)
        for b in range(N_BUCKETS):
            acc = jnp.where(bucket == b, rb_ref[b, h], acc)
        g, hh = divmod(h, Q_PER_KV)
        out_ref[0, g, :, hh * BLOCK:(hh + 1) * BLOCK] = acc
        out_ref[1, g, :, hh * BLOCK:(hh + 1) * BLOCK] = jnp.where(row >= BLOCK, acc, -jnp.inf)


def _build_bias(rel_bias):
    return pl.pallas_call(
        _bias_kernel,
        out_shape=jax.ShapeDtypeStruct((2, N_KV_HEADS, 2 * BLOCK, Q_PER_KV * BLOCK), F32),
        in_specs=[pl.BlockSpec(memory_space=pltpu.SMEM),
                  pl.BlockSpec(memory_space=pltpu.VMEM)],
        out_specs=pl.BlockSpec(memory_space=pltpu.VMEM),
        name="bias_table",
    )(rel_bias.astype(F32), jnp.asarray(_bucket_table_t()))


def _rms_norm(x, g):
    ms = jnp.mean(x * x, axis=-1, keepdims=True)
    return x * lax.rsqrt(ms + EPS) * g


def _inproj_kernel(x_ref, g_ref, w_ref, q_ref, kv_ref, hg_ref):
    hn = _rms_norm(x_ref[...], g_ref[...]).astype(BF16)
    q = jnp.dot(hn, w_ref[:, 0:D_Q], preferred_element_type=F32)
    q_ref[...] = (q * Q_SCALE).astype(BF16)
    kv = jnp.dot(hn, w_ref[:, D_Q:D_Q + 2 * D_KV], preferred_element_type=F32)
    kv_ref[...] = kv.astype(BF16)
    u0 = D_Q + 2 * D_KV
    a = jnp.dot(hn, w_ref[:, u0:u0 + D_CONV], preferred_element_type=F32)
    gate = jnp.dot(hn, w_ref[:, u0 + D_CONV:u0 + 2 * D_CONV], preferred_element_type=F32)
    hg_ref[...] = a * jax.nn.sigmoid(gate)


def _in_proj(x2, g, w_in):
    t = x2.shape[0]
    const = lambda i: (0, 0)
    return pl.pallas_call(
        _inproj_kernel,
        grid=(t // TM_IN,),
        in_specs=[pl.BlockSpec((TM_IN, D_MODEL), lambda i: (i, 0)),
                  pl.BlockSpec((1, D_MODEL), const),
                  pl.BlockSpec((D_MODEL, D_IN), const)],
        out_specs=[pl.BlockSpec((TM_IN, D_Q), lambda i: (i, 0)),
                   pl.BlockSpec((TM_IN, 2 * D_KV), lambda i: (i, 0)),
                   pl.BlockSpec((TM_IN, D_CONV), lambda i: (i, 0))],
        out_shape=[jax.ShapeDtypeStruct((t, D_Q), BF16),
                   jax.ShapeDtypeStruct((t, 2 * D_KV), BF16),
                   jax.ShapeDtypeStruct((t, D_CONV), F32)],
        compiler_params=pltpu.CompilerParams(
            dimension_semantics=("arbitrary",), vmem_limit_bytes=VMEM_LIMIT),
        name="in_proj",
    )(x2, g, w_in)


def _conv_chunk(hg_s, dw_ref, base, lanes):
    acc = None
    for r in range(SUBLANES):
        rows = CONV_ROWS if r == 0 else CONV_ROWS + SUBLANES
        grp = None
        for u in range(2, CONV_WIDTH + 2):
            if u % SUBLANES != r:
                continue
            a8 = u - r
            term = dw_ref[u - 2:u - 1, lanes] * hg_s[base + a8:base + a8 + rows, lanes]
            grp = term if grp is None else grp + term
        part = grp if r == 0 else grp[r:r + CONV_ROWS]
        acc = part if acc is None else acc + part
    return acc


def _make_mixer_kernel(tiles_per_seq):
    n_blk = TS_MIX // BLOCK
    n_chunk = TS_MIX // CONV_ROWS
    qw = Q_PER_KV * BLOCK

    def kernel(sink_ref, q_ref, kvc_ref, kvp_ref, hgc_ref, hgp_ref, x_ref, bias_ref,
               dw_ref, dwb_ref, lng_ref, lnb_ref, wpw_ref, wout_ref, g_ref, o_ref,
               hg_s, conv_s, act_s, attn_s, vt_s):
        first = (pl.program_id(0) % tiles_per_seq) == 0
        sel0 = first.astype(jnp.int32)

        vt_s[:, 0:BLOCK] = kvp_ref[:, D_KV:].astype(F32).T.astype(BF16)
        vt_s[:, BLOCK:] = kvc_ref[:, D_KV:].astype(F32).T.astype(BF16)

        sink_rows = [
            jnp.concatenate([jnp.full((1, BLOCK), sink_ref[g * Q_PER_KV + h], F32)
                             for h in range(Q_PER_KV)], axis=1)
            for g in range(N_KV_HEADS)]

        for j in range(n_blk):
            rows = slice(j * BLOCK, (j + 1) * BLOCK)
            q_blk = q_ref[rows, :]
            if j == 0:
                k_band = jnp.concatenate([kvp_ref[:, 0:D_KV], kvc_ref[0:BLOCK, 0:D_KV]], axis=0)
                sel = sel0
            else:
                k_band = kvc_ref[(j - 1) * BLOCK:(j + 1) * BLOCK, 0:D_KV]
                sel = 0
            vt_band = vt_s[:, j * BLOCK:(j + 2) * BLOCK]
            o_parts = []
            for g in range(N_KV_HEADS):
                h0 = g * Q_PER_KV
                kb = k_band[:, g * HEAD_DIM:(g + 1) * HEAD_DIM]
                q4 = jnp.concatenate(
                    [q_blk[:, (h0 + h) * HEAD_DIM:(h0 + h + 1) * HEAD_DIM]
                     for h in range(Q_PER_KV)], axis=0)
                st = lax.dot_general(kb, q4, (((1,), (1,)), ((), ())),
                                     preferred_element_type=F32)
                st = st + bias_ref[sel, g]
                m = jnp.maximum(jnp.max(st, axis=0, keepdims=True), sink_rows[g])
                p = jnp.exp(st - m)
                denom = jnp.sum(p, axis=0, keepdims=True) + jnp.exp(sink_rows[g] - m)
                ot = jnp.dot(vt_band[g * HEAD_DIM:(g + 1) * HEAD_DIM, :], p.astype(BF16),
                             preferred_element_type=F32)
                ot = ot * (1.0 / denom)
                o_parts += [ot[:, h * BLOCK:(h + 1) * BLOCK] for h in range(Q_PER_KV)]
            attn_s[rows, :] = jnp.concatenate(o_parts, axis=0).T.astype(BF16)

        hg_s[0:CONV_HALO, :] = jnp.where(first, 0.0, hgp_ref[...])
        hg_s[CONV_HALO:CONV_HALO + TS_MIX, :] = hgc_ref[...]
        hg_s[CONV_HALO + TS_MIX:, :] = jnp.zeros((SUBLANES, D_CONV), F32)
        for c in range(n_chunk):
            r0 = c * CONV_ROWS
            for l in range(D_CONV // LANES):
                lanes = slice(l * LANES, (l + 1) * LANES)
                conv_s[r0:r0 + CONV_ROWS, lanes] = (
                    _conv_chunk(hg_s, dw_ref, r0, lanes) + dwb_ref[:, lanes])
            acc = conv_s[r0:r0 + CONV_ROWS, :]
            mu = jnp.mean(acc, axis=-1, keepdims=True)
            cen = acc - mu
            var = jnp.mean(cen * cen, axis=-1, keepdims=True)
            y = cen * lax.rsqrt(var + EPS) * lng_ref[...] + lnb_ref[...]
            act_s[r0:r0 + CONV_ROWS, :] = (y * jax.nn.sigmoid(y)).astype(BF16)

        conv_out = jnp.dot(act_s[...], wpw_ref[...], preferred_element_type=F32)
        mixed = (jnp.dot(attn_s[...], wout_ref[0:D_Q, :], preferred_element_type=F32)
                 + jnp.dot(conv_out.astype(BF16), wout_ref[D_Q:, :], preferred_element_type=F32))
        o_ref[...] = x_ref[...] + _rms_norm(mixed, g_ref[...])

    return kernel


def _mixer(sinks, q, kv, hg, x2, bias, dw, dwb, lng, lnb, wpw, wout, g, seq_len):
    t = x2.shape[0]
    tiles_per_seq = seq_len // TS_MIX
    const2 = lambda i: (0, 0)
    kv_prev = lambda i: (jnp.maximum(i * (TS_MIX // BLOCK) - 1, 0), 0)
    hg_prev = lambda i: (jnp.maximum(i * (TS_MIX // CONV_HALO) - 1, 0), 0)
    return pl.pallas_call(
        _make_mixer_kernel(tiles_per_seq),
        grid=(t // TS_MIX,),
        in_specs=[pl.BlockSpec(memory_space=pltpu.SMEM),
                  pl.BlockSpec((TS_MIX, D_Q), lambda i: (i, 0)),
                  pl.BlockSpec((TS_MIX, 2 * D_KV), lambda i: (i, 0)),
                  pl.BlockSpec((BLOCK, 2 * D_KV), kv_prev),
                  pl.BlockSpec((TS_MIX, D_CONV), lambda i: (i, 0)),
                  pl.BlockSpec((CONV_HALO, D_CONV), hg_prev),
                  pl.BlockSpec((TS_MIX, D_MODEL), lambda i: (i, 0)),
                  pl.BlockSpec((2, N_KV_HEADS, 2 * BLOCK, Q_PER_KV * BLOCK),
                               lambda i: (0, 0, 0, 0)),
                  pl.BlockSpec((CONV_WIDTH, D_CONV), const2),
                  pl.BlockSpec((1, D_CONV), const2),
                  pl.BlockSpec((1, D_CONV), const2),
                  pl.BlockSpec((1, D_CONV), const2),
                  pl.BlockSpec((D_CONV, D_CONV), const2),
                  pl.BlockSpec((D_MODEL, D_MODEL), const2),
                  pl.BlockSpec((1, D_MODEL), const2)],
        out_specs=pl.BlockSpec((TS_MIX, D_MODEL), lambda i: (i, 0)),
        out_shape=jax.ShapeDtypeStruct((t, D_MODEL), F32),
        scratch_shapes=[pltpu.VMEM((CONV_HALO + TS_MIX + SUBLANES, D_CONV), F32),
                        pltpu.VMEM((TS_MIX, D_CONV), F32),
                        pltpu.VMEM((TS_MIX, D_CONV), BF16),
                        pltpu.VMEM((TS_MIX, D_Q), BF16),
                        pltpu.VMEM((2 * HEAD_DIM, BLOCK + TS_MIX), BF16)],
        compiler_params=pltpu.CompilerParams(
            dimension_semantics=("arbitrary",), vmem_limit_bytes=VMEM_LIMIT),
        name="mixer",
    )(sinks, q, kv, kv, hg, hg, x2, bias, dw, dwb, lng, lnb, wpw, wout, g)


def _ffn_kernel(h_ref, gpre_ref, gpost_ref, wg_ref, wu_ref, wd_ref, o_ref, a_s):
    h = h_ref[...]
    hn = _rms_norm(h, gpre_ref[...]).astype(BF16)
    for c in range(D_FF // FF_CHUNK):
        cols = slice(c * FF_CHUNK, (c + 1) * FF_CHUNK)
        gt = jnp.dot(hn, wg_ref[:, cols], preferred_element_type=F32)
        up = jnp.dot(hn, wu_ref[:, cols], preferred_element_type=F32)
        a_s[:, cols] = (gt * jax.nn.sigmoid(gt) * up).astype(BF16)
    ff = jnp.dot(a_s[...], wd_ref[...], preferred_element_type=F32)
    o_ref[...] = h + _rms_norm(ff, gpost_ref[...])


def _ffn(h1, gpre, gpost, wg, wu, wd):
    t = h1.shape[0]
    const = lambda i: (0, 0)
    once = pl.Buffered(1)
    return pl.pallas_call(
        _ffn_kernel,
        grid=(t // TM_FFN,),
        in_specs=[pl.BlockSpec((TM_FFN, D_MODEL), lambda i: (i, 0)),
                  pl.BlockSpec((1, D_MODEL), const),
                  pl.BlockSpec((1, D_MODEL), const),
                  pl.BlockSpec((D_MODEL, D_FF), const, pipeline_mode=once),
                  pl.BlockSpec((D_MODEL, D_FF), const, pipeline_mode=once),
                  pl.BlockSpec((D_FF, D_MODEL), const, pipeline_mode=once)],
        out_specs=pl.BlockSpec((TM_FFN, D_MODEL), lambda i: (i, 0)),
        out_shape=jax.ShapeDtypeStruct((t, D_MODEL), F32),
        scratch_shapes=[pltpu.VMEM((TM_FFN, D_FF), BF16)],
        compiler_params=pltpu.CompilerParams(
            dimension_semantics=("arbitrary",), vmem_limit_bytes=VMEM_LIMIT),
        name="ffn",
    )(h1, gpre, gpost, wg, wu, wd)


def kernel(x, mix_pre_g, mix_post_g, ffn_pre_g, ffn_post_g, w_in, conv_dw, conv_dw_b,
           conv_ln_g, conv_ln_b, w_conv_pw, attn_sinks, rel_bias, w_out,
           w_gate, w_up, w_down):
    b, s, d = x.shape
    assert d == D_MODEL and s % TS_MIX == 0 and (b * s) % TM_IN == 0 and (b * s) % TM_FFN == 0
    x2 = x.reshape(b * s, d)
    row = lambda v: v.reshape(1, -1).astype(F32)

    bias = _build_bias(rel_bias)
    q, kv, hg = _in_proj(x2, row(mix_pre_g), w_in.astype(BF16))
    h1 = _mixer(attn_sinks.astype(F32), q, kv, hg, x2, bias,
                conv_dw.astype(F32), row(conv_dw_b), row(conv_ln_g), row(conv_ln_b),
                w_conv_pw.astype(BF16), w_out.astype(BF16), row(mix_post_g), s)
    out = _ffn(h1, row(ffn_pre_g), row(ffn_post_g),
               w_gate.astype(BF16), w_up.astype(BF16), w_down.astype(BF16))
    return out.reshape(b, s, d)
```
